```python
import math
import jax
import jax.numpy as jnp
from jax import lax
import numpy as np

D_MODEL = 1024
BATCH = 32
SEQ = 2048
DEPTH = 2
DEC_BATCH = 32
DEC_SEQ = 16
PAST_LEN = 4096

CHUNK = 64
EPS = 1e-6
PLE_DIM = 256
NEG_INF = -1e30

WINDOW = 128
N_WIN_CHUNKS = WINDOW // CHUNK
ATTN_HEADS = 8
ATTN_KV_HEADS = 2
ATTN_GROUP = ATTN_HEADS // ATTN_KV_HEADS
HEAD_DIM = 64
ROPE_THETA = 10000.0
Q_DIM = ATTN_HEADS * HEAD_DIM
KV_DIM = ATTN_KV_HEADS * HEAD_DIM
WIN_ROWS = min(WINDOW, PAST_LEN)

SSD_HEADS = 16
SSD_HEAD_DIM = 64
SSD_INNER = SSD_HEADS * SSD_HEAD_DIM
SSD_GROUPS = 2
SSD_HPG = SSD_HEADS // SSD_GROUPS
SSD_STATE = 128
SSD_CONV = 4
SSD_BC = SSD_GROUPS * SSD_STATE
SSD_CONV_DIM = SSD_INNER + 2 * SSD_BC

SC_CONV = 3

HYB_IN = Q_DIM + 2 * KV_DIM + SSD_INNER + SSD_CONV_DIM + SSD_HEADS
HYB_OUT = Q_DIM + SSD_INNER
HYB_SPLITS = (Q_DIM, Q_DIM + KV_DIM, Q_DIM + 2 * KV_DIM, Q_DIM + 2 * KV_DIM + SSD_INNER, Q_DIM + 2 * KV_DIM + SSD_INNER + SSD_CONV_DIM)

FF_RAW = -(-8 * D_MODEL // 3)
D_FF = -(-FF_RAW // 256) * 256

kernel_name = 'hybrid_stream_encoder_step'


def rmsnorm(x, w):
    x32 = x.astype(jnp.float32)
    y = x32 * lax.rsqrt(jnp.mean(x32 * x32, axis=-1, keepdims=True) + EPS)
    return (y * w.astype(jnp.float32)).astype(x.dtype)


def rope(x, pos):
    half = HEAD_DIM // 2
    inv = ROPE_THETA ** (-jnp.arange(half, dtype=jnp.float32) / half)
    ang = pos.astype(jnp.float32)[:, None] * inv[None, :]
    cos = jnp.cos(ang)[:, None, :]
    sin = jnp.sin(ang)[:, None, :]
    x32 = x.astype(jnp.float32)
    x1, x2 = x32[..., :half], x32[..., half:]
    return jnp.concatenate([x1 * cos - x2 * sin, x2 * cos + x1 * sin], axis=-1).astype(x.dtype)


def causal_dwconv(u, w, prev):
    K = w.shape[0]
    L = u.shape[1]
    full = jnp.concatenate([prev.astype(u.dtype), u], axis=1)
    out = full[:, 0:L] * w[0]
    for j in range(1, K):
        out = out + full[:, j:j + L] * w[j]
    return out, full[:, L:]


def sink_attention(q, k, v, sinks, valid):
    s = jnp.einsum('bnqkgd,bnskd->bnkgqs', q.astype(jnp.float32), k.astype(jnp.float32)) * (HEAD_DIM ** -0.5)
    if valid is not None:
        s = jnp.where(valid[None, :, None, None, None, :], s, NEG_INF)
    sink = sinks.astype(jnp.float32).reshape(ATTN_KV_HEADS, ATTN_GROUP)[None, None, :, :, None]
    m = jnp.maximum(s.max(axis=-1), sink)
    p = jnp.exp(s - m[..., None])
    denom = p.sum(axis=-1) + jnp.exp(sink - m)
    o = jnp.einsum('bnkgqs,bnskd->bnqkgd', p, v.astype(jnp.float32))
    o = o / jnp.moveaxis(denom, -1, 2)[..., None]
    return o.astype(q.dtype)


def window_attention_prompt(q, k, v, sinks):
    b, L = q.shape[:2]
    nc = L // CHUNK
    qb = q.reshape(b, nc, CHUNK, ATTN_KV_HEADS, ATTN_GROUP, HEAD_DIM)
    pad = ((0, 0), (N_WIN_CHUNKS * CHUNK, 0), (0, 0), (0, 0))
    kp = jnp.pad(k, pad).reshape(b, nc + N_WIN_CHUNKS, CHUNK, ATTN_KV_HEADS, HEAD_DIM)
    vp = jnp.pad(v, pad).reshape(b, nc + N_WIN_CHUNKS, CHUNK, ATTN_KV_HEADS, HEAD_DIM)
    kb = jnp.concatenate([kp[:, j:j + nc] for j in range(N_WIN_CHUNKS + 1)], axis=2)
    vb = jnp.concatenate([vp[:, j:j + nc] for j in range(N_WIN_CHUNKS + 1)], axis=2)
    key_chunk = jnp.arange(nc)[:, None] - N_WIN_CHUNKS + jnp.repeat(jnp.arange(N_WIN_CHUNKS + 1), CHUNK)[None, :]
    o = sink_attention(qb, kb, vb, sinks, key_chunk >= 0)
    return o.reshape(b, L, Q_DIM)


def window_attention_sample(q, k, v, k_cache, v_cache, sinks):
    b, L = q.shape[:2]
    qb = q.reshape(b, 1, L, ATTN_KV_HEADS, ATTN_GROUP, HEAD_DIM)
    kb = jnp.concatenate([k_cache.astype(k.dtype), k], axis=1)[:, None]
    vb = jnp.concatenate([v_cache.astype(v.dtype), v], axis=1)[:, None]
    o = sink_attention(qb, kb, vb, sinks, None)
    return o.reshape(b, L, Q_DIM)


def ssd_scan(x, dt, a, bm, cm, h0):
    b, L = x.shape[:2]
    T = min(CHUNK, L)
    nc = L // T

    def blk(t):
        return t.reshape((b, nc, T) + t.shape[2:])

    x, dt, bm, cm = blk(x), blk(dt), blk(bm), blk(cm)
    cum = jnp.cumsum(dt * a, axis=2)
    seg = cum[:, :, :, None] - cum[:, :, None, :]
    causal = jnp.tril(jnp.ones((T, T), dtype=bool))[:, :, None, None]
    decay = jnp.where(causal, jnp.exp(jnp.where(causal, seg, 0.0)), 0.0)
    cb = jnp.einsum('bctgn,bcsgn->bctsg', cm, bm)
    w_ts = cb[..., None] * decay * dt[:, :, None]
    y_intra = jnp.einsum('bctsgh,bcsghp->bctghp', w_ts, x)
    last = cum[:, :, -1]
    w_state = jnp.exp(last[:, :, None] - cum) * dt
    s_chunk = jnp.einsum('bcsgh,bcsghp,bcsgn->bcghpn', w_state, x, bm)

    def step(h, inp):
        dec, s = inp
        return jnp.exp(dec)[..., None, None] * h + s, h

    h_final, h_prev = lax.scan(step, h0, (jnp.moveaxis(last, 1, 0), jnp.moveaxis(s_chunk, 1, 0)))
    h_prev = jnp.moveaxis(h_prev, 0, 1)
    y_inter = jnp.einsum('bctgn,bcghpn->bctghp', cm, h_prev) * jnp.exp(cum)[..., None]
    y = (y_intra + y_inter).reshape(b, L, SSD_GROUPS, SSD_HPG, SSD_HEAD_DIM)
    return y, h_final


def ssd_mixer(z, xbc, dt_raw, h0, conv0, conv_w, conv_b, dt_bias, a_log, d_skip, norm_w):
    b, L, _ = z.shape
    xbc, conv_state = causal_dwconv(xbc, conv_w, conv0)
    xbc = jax.nn.silu(xbc + conv_b).astype(jnp.float32)
    xs = xbc[..., :SSD_INNER].reshape(b, L, SSD_GROUPS, SSD_HPG, SSD_HEAD_DIM)
    bm = xbc[..., SSD_INNER:SSD_INNER + SSD_BC].reshape(b, L, SSD_GROUPS, SSD_STATE)
    cm = xbc[..., SSD_INNER + SSD_BC:].reshape(b, L, SSD_GROUPS, SSD_STATE)
    dt = jax.nn.softplus(dt_raw.astype(jnp.float32) + dt_bias.astype(jnp.float32)).reshape(b, L, SSD_GROUPS, SSD_HPG)
    a = -jnp.exp(a_log.astype(jnp.float32)).reshape(SSD_GROUPS, SSD_HPG)
    h0 = h0.astype(jnp.float32).reshape(b, SSD_GROUPS, SSD_HPG, SSD_HEAD_DIM, SSD_STATE)
    y, h = ssd_scan(xs, dt, a, bm, cm, h0)
    y = y + d_skip.astype(jnp.float32).reshape(SSD_GROUPS, SSD_HPG)[..., None] * xs
    gw = SSD_HPG * SSD_HEAD_DIM
    y = y.reshape(b, L, SSD_GROUPS, gw) * jax.nn.silu(z.astype(jnp.float32).reshape(b, L, SSD_GROUPS, gw))
    y = y * lax.rsqrt(jnp.mean(y * y, axis=-1, keepdims=True) + EPS) * norm_w.astype(jnp.float32).reshape(SSD_GROUPS, gw)
    return y.reshape(b, L, SSD_INNER).astype(z.dtype), conv_state, h.reshape(b, SSD_HEADS, SSD_HEAD_DIM, SSD_STATE)


def hybrid_mixer(xn, pos, win_k, win_v, ssm0, conv0, w_in, w_out, sinks, conv_w, conv_b, dt_bias, a_log, d_skip, norm_w):
    b, L, _ = xn.shape
    q, k, v, z, xbc, dt_raw = jnp.split(xn @ w_in, HYB_SPLITS, axis=-1)
    q = rope(q.reshape(b, L, ATTN_HEADS, HEAD_DIM), pos)
    k = rope(k.reshape(b, L, ATTN_KV_HEADS, HEAD_DIM), pos)
    v = v.reshape(b, L, ATTN_KV_HEADS, HEAD_DIM)
    if win_k is None:
        attn = window_attention_prompt(q, k, v, sinks)
        k_state, v_state = k[:, -WIN_ROWS:], v[:, -WIN_ROWS:]
    else:
        attn = window_attention_sample(q, k, v, win_k, win_v, sinks)
        k_state, v_state = k, v
    ssd, conv_state, ssm_state = ssd_mixer(z, xbc, dt_raw, ssm0, conv0, conv_w, conv_b, dt_bias, a_log, d_skip, norm_w)
    out = jnp.concatenate([attn, ssd], axis=-1) @ w_out
    return out, k_state, v_state, ssm_state.astype(xn.dtype), conv_state


def short_conv_mixer(xn, prev, w_in, conv_w, w_out):
    bg, cg, h = jnp.split(xn @ w_in, 3, axis=-1)
    u, state = causal_dwconv(cg * h, conv_w, prev)
    return (bg * u) @ w_out, state


def swiglu(xn, w_gate, w_up, w_down):
    return (jax.nn.silu(xn @ w_gate) * (xn @ w_up)) @ w_down


def trunk(x, p, pos0, win_k, win_v, ssm0, ssd_conv0, sc_conv0,
          norm_mix, norm_ffn, norm_ple, final_norm,
          hyb_w_in, hyb_w_out, attn_sinks, ssd_conv_w, ssd_conv_b, ssd_dt_bias, ssd_a_log, ssd_d, ssd_norm_w,
          sc_w_in, sc_conv_w, sc_w_out,
          ffn_w_gate, ffn_w_up, ffn_w_down, ple_w_proj, ple_w_gate):
    pos = pos0 + jnp.arange(x.shape[1], dtype=jnp.int32)
    for i in range(DEPTH):
        xn = rmsnorm(x, norm_mix[i])
        if i % 2 == 0:
            mix, k_state, v_state, ssm_state, ssd_conv_state = hybrid_mixer(
                xn, pos, win_k, win_v, ssm0, ssd_conv0, hyb_w_in, hyb_w_out, attn_sinks,
                ssd_conv_w, ssd_conv_b, ssd_dt_bias, ssd_a_log, ssd_d, ssd_norm_w)
        else:
            mix, sc_state = short_conv_mixer(xn, sc_conv0, sc_w_in, sc_conv_w, sc_w_out)
        x = x + mix
        x = x + swiglu(rmsnorm(x, norm_ffn[i]), ffn_w_gate[i], ffn_w_up[i], ffn_w_down[i])
        gate = jax.nn.sigmoid(rmsnorm(x, norm_ple[i]) @ ple_w_gate[i])
        x = x + gate * (p[i].astype(x.dtype) @ ple_w_proj[i])
    return rmsnorm(x, final_norm), k_state, v_state, ssm_state, ssd_conv_state, sc_state


def setup_inputs(seed: int = 0) -> dict:
    key = jax.random.key(seed)
    ks = jax.random.split(key, 30)
    f32 = jnp.float32

    def nrm(k, shape, scale):
        return jax.random.normal(k, shape, f32) * scale

    dt0 = jnp.exp(jax.random.uniform(ks[18], (SSD_HEADS,), f32, math.log(1e-3), math.log(1e-1)))
    return {
        'x_prompt': nrm(ks[0], (BATCH, SEQ, D_MODEL), 1.0),
        'x_sample': nrm(ks[1], (DEC_BATCH, DEC_SEQ, D_MODEL), 1.0),
        'p_prompt': nrm(ks[2], (DEPTH, BATCH, SEQ, PLE_DIM), 1.0),
        'p_sample': nrm(ks[3], (DEPTH, DEC_BATCH, DEC_SEQ, PLE_DIM), 1.0),
        'cache_win_k': nrm(ks[4], (DEC_BATCH, WIN_ROWS, ATTN_KV_HEADS, HEAD_DIM), 1.0),
        'cache_win_v': nrm(ks[5], (DEC_BATCH, WIN_ROWS, ATTN_KV_HEADS, HEAD_DIM), 1.0),
        'state_ssm': nrm(ks[6], (DEC_BATCH, SSD_HEADS, SSD_HEAD_DIM, SSD_STATE), 0.5),
        'state_ssd_conv': nrm(ks[7], (DEC_BATCH, SSD_CONV - 1, SSD_CONV_DIM), 1.0),
        'state_short_conv': nrm(ks[8], (DEC_BATCH, SC_CONV - 1, D_MODEL), 1.0),
        'norm_mix': 1.0 + nrm(ks[9], (DEPTH, D_MODEL), 0.02),
        'norm_ffn': 1.0 + nrm(ks[10], (DEPTH, D_MODEL), 0.02),
        'norm_ple': 1.0 + nrm(ks[11], (DEPTH, D_MODEL), 0.02),
        'final_norm': 1.0 + nrm(ks[12], (D_MODEL,), 0.02),
        'hyb_w_in': nrm(ks[13], (D_MODEL, HYB_IN), D_MODEL ** -0.5),
        'hyb_w_out': nrm(ks[14], (HYB_OUT, D_MODEL), HYB_OUT ** -0.5),
        'attn_sinks': nrm(ks[15], (ATTN_HEADS,), 1.0),
        'ssd_conv_w': nrm(ks[16], (SSD_CONV, SSD_CONV_DIM), SSD_CONV ** -0.5),
        'ssd_conv_b': nrm(ks[17], (SSD_CONV_DIM,), 0.01),
        'ssd_dt_bias': dt0 + jnp.log(-jnp.expm1(-dt0)),
        'ssd_a_log': jnp.log(jax.random.uniform(ks[19], (SSD_HEADS,), f32, 1.0, 16.0)),
        'ssd_d': 1.0 + nrm(ks[20], (SSD_HEADS,), 0.02),
        'ssd_norm_w': 1.0 + nrm(ks[21], (SSD_INNER,), 0.02),
        'sc_w_in': nrm(ks[22], (D_MODEL, 3 * D_MODEL), D_MODEL ** -0.5),
        'sc_conv_w': nrm(ks[23], (SC_CONV, D_MODEL), SC_CONV ** -0.5),
        'sc_w_out': nrm(ks[24], (D_MODEL, D_MODEL), D_MODEL ** -0.5),
        'ffn_w_gate': nrm(ks[25], (DEPTH, D_MODEL, D_FF), D_MODEL ** -0.5),
        'ffn_w_up': nrm(ks[26], (DEPTH, D_MODEL, D_FF), D_MODEL ** -0.5),
        'ffn_w_down': nrm(ks[27], (DEPTH, D_FF, D_MODEL), D_FF ** -0.5),
        'ple_w_proj': nrm(ks[28], (DEPTH, PLE_DIM, D_MODEL), PLE_DIM ** -0.5),
        'ple_w_gate': nrm(ks[29], (DEPTH, D_MODEL, D_MODEL), D_MODEL ** -0.5),
    }


def reference(x_prompt, x_sample, p_prompt, p_sample, cache_win_k, cache_win_v, state_ssm, state_ssd_conv, state_short_conv,
              norm_mix, norm_ffn, norm_ple, final_norm,
              hyb_w_in, hyb_w_out, attn_sinks, ssd_conv_w, ssd_conv_b, ssd_dt_bias, ssd_a_log, ssd_d, ssd_norm_w,
              sc_w_in, sc_conv_w, sc_w_out,
              ffn_w_gate, ffn_w_up, ffn_w_down, ple_w_proj, ple_w_gate):
    weights = (norm_mix, norm_ffn, norm_ple, final_norm,
               hyb_w_in, hyb_w_out, attn_sinks, ssd_conv_w, ssd_conv_b, ssd_dt_bias, ssd_a_log, ssd_d, ssd_norm_w,
               sc_w_in, sc_conv_w, sc_w_out,
               ffn_w_gate, ffn_w_up, ffn_w_down, ple_w_proj, ple_w_gate)
    bp = x_prompt.shape[0]
    dtp = x_prompt.dtype
    ssm0 = jnp.zeros((bp, SSD_HEADS, SSD_HEAD_DIM, SSD_STATE), dtp)
    ssd_conv0 = jnp.zeros((bp, SSD_CONV - 1, SSD_CONV_DIM), dtp)
    sc_conv0 = jnp.zeros((bp, SC_CONV - 1, D_MODEL), dtp)
    y_prompt, pk, pv, pssm, pconv, psc = trunk(x_prompt, p_prompt, 0, None, None, ssm0, ssd_conv0, sc_conv0, *weights)
    y_sample, sk, sv, sssm, sconv, ssc = trunk(x_sample, p_sample, PAST_LEN, cache_win_k, cache_win_v,
                                               state_ssm, state_ssd_conv, state_short_conv, *weights)
    return (y_prompt, y_sample, pk, pv, pssm, pconv, psc, sk, sv, sssm, sconv, ssc)
```

```python
import functools

import jax
import jax.numpy as jnp
from jax import lax
from jax.experimental import pallas as pl
from jax.experimental.pallas import tpu as pltpu

F32 = jnp.float32
BF16 = jnp.bfloat16

D_MODEL = 1024
CHUNK = 64
EPS = 1e-6
PLE_DIM = 256
NEG_INF = -1e30
PAST_LEN = 4096
WINDOW = 128
ATTN_HEADS = 8
ATTN_KV_HEADS = 2
ATTN_GROUP = ATTN_HEADS // ATTN_KV_HEADS
HEAD_DIM = 64
ROPE_THETA = 10000.0
Q_DIM = ATTN_HEADS * HEAD_DIM
KV_DIM = ATTN_KV_HEADS * HEAD_DIM
SSD_HEADS = 16
SSD_HEAD_DIM = 64
SSD_INNER = SSD_HEADS * SSD_HEAD_DIM
SSD_GROUPS = 2
SSD_GROUP_W = SSD_INNER // SSD_GROUPS
SSD_STATE = 128
SSD_CONV = 4
SSD_BC = SSD_GROUPS * SSD_STATE
SSD_CONV_DIM = SSD_INNER + 2 * SSD_BC
SC_CONV = 3
D_FF = 2816

LANES = 128
SUBLANES = 8
VMEM_LIMIT_BYTES = 56 * 1024 * 1024

QK_DIM = Q_DIM + KV_DIM
COL_V = QK_DIM
COL_Z = COL_V + KV_DIM
COL_XBC = COL_Z + SSD_INNER
COL_DT = COL_XBC + SSD_CONV_DIM
HYB_IN_PAD = COL_DT + LANES

TOKEN_TILE = 512


def _params(semantics):
    return pltpu.CompilerParams(dimension_semantics=semantics, vmem_limit_bytes=VMEM_LIMIT_BYTES)


def _const_spec(shape):
    zeros = (0,) * len(shape)
    return pl.BlockSpec(shape, lambda *_: zeros, pipeline_mode=pl.Buffered(1))


def _rms(x, w):
    return x * lax.rsqrt(jnp.mean(x * x, axis=-1, keepdims=True) + EPS) * w


def _sigmoid(x):
    return 1.0 / (1.0 + jnp.exp(-x))


def _silu(x):
    return x * _sigmoid(x)


def _softplus(x):
    return jnp.maximum(x, 0.0) + jnp.log1p(jnp.exp(-jnp.abs(x)))


def _dot(a, b):
    return jnp.dot(a, b, preferred_element_type=F32)


def _dot_nt(a, b):
    return lax.dot_general(a, b, (((1,), (1,)), ((), ())), preferred_element_type=F32)


def _split3(x):
    h1 = x.astype(BF16)
    r1 = x - h1.astype(F32)
    h2 = r1.astype(BF16)
    h3 = (r1 - h2.astype(F32)).astype(BF16)
    return h1, h2, h3


def _dot_exact_rhs(x, rhs_bf16):
    h1, h2, h3 = _split3(x)
    return _dot(h1, rhs_bf16) + _dot(h2, rhs_bf16) + _dot(h3, rhs_bf16)


def _dot_exact_lhs(lhs_bf16, x):
    h1, h2, h3 = _split3(x)
    return _dot(lhs_bf16, h1) + _dot(lhs_bf16, h2) + _dot(lhs_bf16, h3)


def _rope_table_kernel(inv_ref, cos_ref, sin_ref, *, pos0, period):
    rows = cos_ref.shape[0]
    row = lax.broadcasted_iota(jnp.int32, (rows, LANES), 0)
    lane = lax.broadcasted_iota(jnp.int32, (rows, LANES), 1)
    pos = (pos0 + lax.rem(row, period)).astype(F32)
    ang = pos * inv_ref[...]
    first_half = lax.rem(lane, HEAD_DIM) < HEAD_DIM // 2
    cos_ref[...] = jnp.cos(ang)
    sin_ref[...] = jnp.where(first_half, -jnp.sin(ang), jnp.sin(ang))


def _rope_tables(rows, pos0, period):
    half = HEAD_DIM // 2
    inv = ROPE_THETA ** (-jnp.arange(half, dtype=F32) / half)
    inv = jnp.tile(inv, LANES // half).reshape(1, LANES)
    return pl.pallas_call(
        functools.partial(_rope_table_kernel, pos0=pos0, period=period),
        out_shape=(jax.ShapeDtypeStruct((rows, LANES), F32),) * 2,
        name="rope_tables",
    )(inv)


def _inproj_kernel(x_ref, nw_ref, w_ref, cos_ref, sin_ref, q_ref, k_ref, v_ref, z_ref, xbc_ref, dt_ref):
    xn = _rms(x_ref[...], nw_ref[...]).astype(BF16)
    y = _dot(xn, w_ref[...])
    cos = cos_ref[...]
    sin = sin_ref[...]
    lane = lax.broadcasted_iota(jnp.int32, cos.shape, 1)
    first_half = lax.rem(lane, HEAD_DIM) < HEAD_DIM // 2
    half = HEAD_DIM // 2
    for j in range(QK_DIM // LANES):
        blk = y[:, j * LANES:(j + 1) * LANES]
        partner = jnp.where(first_half, pltpu.roll(blk, LANES - half, 1), pltpu.roll(blk, half, 1))
        roped = blk * cos + partner * sin
        if j < Q_DIM // LANES:
            q_ref[:, j * LANES:(j + 1) * LANES] = (roped * (HEAD_DIM ** -0.5)).astype(BF16)
        else:
            k_ref[...] = roped
    v_ref[...] = y[:, COL_V:COL_Z]
    z_ref[...] = y[:, COL_Z:COL_XBC]
    xbc_ref[...] = y[:, COL_XBC:COL_DT]
    dt_ref[...] = y[:, COL_DT:HYB_IN_PAD]


def _inproj(x, norm_w, w_pad, cos_t, sin_t, tm):
    n = x.shape[0]
    table_blocks = cos_t.shape[0] // tm
    row = lambda i: (i, 0)
    tab = lambda i: (i % table_blocks, 0)
    outs = (
        jax.ShapeDtypeStruct((n, Q_DIM), BF16),
        jax.ShapeDtypeStruct((n, KV_DIM), F32),
        jax.ShapeDtypeStruct((n, KV_DIM), F32),
        jax.ShapeDtypeStruct((n, SSD_INNER), F32),
        jax.ShapeDtypeStruct((n, SSD_CONV_DIM), F32),
        jax.ShapeDtypeStruct((n, LANES), F32),
    )
    return pl.pallas_call(
        _inproj_kernel,
        grid=(n // tm,),
        in_specs=[
            pl.BlockSpec((tm, D_MODEL), row),
            _const_spec((1, D_MODEL)),
            _const_spec((D_MODEL, HYB_IN_PAD)),
            pl.BlockSpec((tm, LANES), tab),
            pl.BlockSpec((tm, LANES), tab),
        ],
        out_specs=[pl.BlockSpec((tm, o.shape[1]), row) for o in outs],
        out_shape=outs,
        compiler_params=_params(("parallel",)),
        name="l0_inproj",
    )(x, norm_w, w_pad, cos_t, sin_t)


def _attn_kernel(q_ref, kp_ref, kc_ref, vp_ref, vc_ref, sink_ref, o_ref, *, chunk, mask_history):
    tq = q_ref.shape[1]
    kcat = jnp.concatenate([kp_ref[0], kc_ref[0]], axis=0).astype(BF16)
    vcat = jnp.concatenate([vp_ref[0], vc_ref[0]], axis=0).astype(BF16)
    win = WINDOW + chunk
    tile_start = pl.program_id(1) * tq
    for c in range(tq // chunk):
        kw = kcat[c * chunk:c * chunk + win]
        vw = vcat[c * chunk:c * chunk + win]
        outs = []
        for kh in range(ATTN_KV_HEADS):
            kk = kw[:, kh * HEAD_DIM:(kh + 1) * HEAD_DIM]
            vv = vw[:, kh * HEAD_DIM:(kh + 1) * HEAD_DIM]
            qs = jnp.concatenate(
                [q_ref[0, c * chunk:(c + 1) * chunk, (kh * ATTN_GROUP + g) * HEAD_DIM:(kh * ATTN_GROUP + g + 1) * HEAD_DIM]
                 for g in range(ATTN_GROUP)], axis=0)
            s = _dot_nt(qs, kk)
            if mask_history:
                key_pos = tile_start + (c * chunk - WINDOW) + lax.broadcasted_iota(jnp.int32, s.shape, 1)
                s = jnp.where(key_pos >= 0, s, NEG_INF)
            sink = sink_ref[kh]
            m = jnp.maximum(jnp.max(s, axis=-1, keepdims=True), sink)
            p = jnp.exp(s - m)
            denom = jnp.sum(p, axis=-1, keepdims=True) + jnp.exp(sink - m)
            o = _dot(p.astype(BF16), vv) / denom
            outs.extend(o[g * chunk:(g + 1) * chunk] for g in range(ATTN_GROUP))
        o_ref[0, c * chunk:(c + 1) * chunk, :] = jnp.concatenate(outs, axis=1).astype(BF16)


def _attention(q, k_prev_src, k, v_prev_src, v, sink_col, tq, chunk, mask_history):
    b, l, _ = q.shape
    per = tq // WINDOW
    if mask_history:
        prev = lambda i, j: (i, jnp.maximum(j * per - 1, 0), 0)
    else:
        prev = lambda i, j: (i, 0, 0)
    cur = lambda i, j: (i, j, 0)
    return pl.pallas_call(
        functools.partial(_attn_kernel, chunk=chunk, mask_history=mask_history),
        grid=(b, l // tq),
        in_specs=[
            pl.BlockSpec((1, tq, Q_DIM), cur),
            pl.BlockSpec((1, WINDOW, KV_DIM), prev),
            pl.BlockSpec((1, tq, KV_DIM), cur),
            pl.BlockSpec((1, WINDOW, KV_DIM), prev),
            pl.BlockSpec((1, tq, KV_DIM), cur),
            _const_spec(sink_col.shape),
        ],
        out_specs=pl.BlockSpec((1, tq, Q_DIM), cur),
        out_shape=jax.ShapeDtypeStruct((b, l, Q_DIM), BF16),
        compiler_params=_params(("parallel", "parallel")),
        name="l0_attention",
    )(q, k_prev_src, k, v_prev_src, v, sink_col)


def _ssd_kernel(*refs, chunk, has_state):
    if has_state:
        (xbc_ref, xprev_ref, z_ref, dt_ref, cw_ref, cb_ref, dtb_ref, alog_ref, alog_e_ref, d_e_ref, nw_ref, e_ref,
         conv0_ref, h0_ref, y_ref, conv_out_ref, h_out_ref, full_s, xc_s, ht_s) = refs
    else:
        (xbc_ref, xprev_ref, z_ref, dt_ref, cw_ref, cb_ref, dtb_ref, alog_ref, alog_e_ref, d_e_ref, nw_ref, e_ref,
         y_ref, conv_out_ref, h_out_ref, full_s, xc_s, ht_s) = refs
    lt = pl.program_id(1)
    n_lt = pl.num_programs(1)
    tl = xbc_ref.shape[1]
    pad = SUBLANES

    @pl.when(lt == 0)
    def _():
        if has_state:
            full_s[0:pad, :] = conv0_ref[0]
            for g in range(SSD_GROUPS):
                ht_s[g] = h0_ref[0, g * SSD_GROUP_W:(g + 1) * SSD_GROUP_W, :].T
        else:
            full_s[0:pad, :] = jnp.zeros((pad, SSD_CONV_DIM), F32)
            ht_s[...] = jnp.zeros(ht_s.shape, F32)

    @pl.when(lt > 0)
    def _():
        full_s[0:pad, :] = xprev_ref[0]

    full_s[pad:pad + tl, :] = xbc_ref[0]
    acc = full_s[pad - (SSD_CONV - 1):pad - (SSD_CONV - 1) + tl, :] * cw_ref[0:1, :]
    for j in range(1, SSD_CONV):
        off = pad - (SSD_CONV - 1) + j
        acc = acc + full_s[off:off + tl, :] * cw_ref[j:j + 1, :]
    xc_s[...] = _silu(acc + cb_ref[...])
    conv_out_ref[0] = full_s[tl:tl + pad, :]

    a_e = -jnp.exp(alog_e_ref[...])
    a_c = -jnp.exp(alog_ref[...])
    d_e = d_e_ref[...]
    nw = nw_ref[...]
    e_mat = e_ref[...]
    dt_bias = dtb_ref[...]
    row_i = lax.broadcasted_iota(jnp.int32, (chunk, chunk), 0)
    col_i = lax.broadcasted_iota(jnp.int32, (chunk, chunk), 1)
    tril_b = (col_i <= row_i).astype(BF16)
    row_p = lax.broadcasted_iota(jnp.int32, (chunk, LANES), 0)
    lane_p = lax.broadcasted_iota(jnp.int32, (chunk, LANES), 1)
    tril_p = lax.rem(lane_p, HEAD_DIM) <= row_p if chunk == HEAD_DIM else None
    lane_lo = lane_p < SSD_HEAD_DIM

    def chunk_body(c, carry):
        r0 = pl.multiple_of(c * chunk, chunk)
        xc = xc_s[pl.ds(r0, chunk), :]
        xs = xc[:, :SSD_INNER]
        bm = xc[:, SSD_INNER:SSD_INNER + SSD_BC]
        cm = xc[:, SSD_INNER + SSD_BC:]
        dt = _softplus(dt_ref[0, pl.ds(r0, chunk), :] + dt_bias)
        dt_e = _dot_exact_rhs(dt, e_mat)
        cum_e = _dot_exact_lhs(tril_b, dt_e * a_e)
        cum_c = _dot_exact_lhs(tril_b, dt * a_c)
        cum_t = cum_c.T
        dt_t = dt.T
        last_e = cum_e[chunk - 1:chunk, :]

        y_parts = []
        for g in range(SSD_GROUPS):
            cm_g = cm[:, g * SSD_STATE:(g + 1) * SSD_STATE].astype(BF16)
            bm_g = bm[:, g * SSD_STATE:(g + 1) * SSD_STATE].astype(BF16)
            cb = _dot_nt(cm_g, bm_g)
            for jj in range(SSD_GROUP_W // LANES):
                j = g * (SSD_GROUP_W // LANES) + jj
                if chunk == HEAD_DIM:
                    cb_p = jnp.concatenate([cb, cb], axis=1)
                    row_c = jnp.concatenate([cum_t[2 * j:2 * j + 1, :], cum_t[2 * j + 1:2 * j + 2, :]], axis=1)
                    row_d = jnp.concatenate([dt_t[2 * j:2 * j + 1, :], dt_t[2 * j + 1:2 * j + 2, :]], axis=1)
                    seg = cum_e[:, j * LANES:(j + 1) * LANES] - row_c
                    dec = jnp.where(tril_p, jnp.exp(jnp.where(tril_p, seg, 0.0)), 0.0)
                    w_p = (cb_p * dec * row_d).astype(BF16)
                    x_p = xs[:, j * LANES:(j + 1) * LANES]
                    zero = jnp.zeros_like(x_p)
                    rhs = jnp.concatenate([jnp.where(lane_lo, x_p, zero), jnp.where(lane_lo, zero, x_p)], axis=0)
                    y_parts.append(_dot(w_p, rhs.astype(BF16)))
                else:
                    for e in range(2):
                        h = 2 * j + e
                        seg = cum_e[:, h * SSD_HEAD_DIM:h * SSD_HEAD_DIM + chunk] - cum_t[h:h + 1, :]
                        low = col_i <= row_i
                        dec = jnp.where(low, jnp.exp(jnp.where(low, seg, 0.0)), 0.0)
                        w_h = (cb * dec * dt_t[h:h + 1, :]).astype(BF16)
                        x_h = xs[:, h * SSD_HEAD_DIM:(h + 1) * SSD_HEAD_DIM].astype(BF16)
                        y_parts.append(_dot(w_h, x_h))
        y_intra = jnp.concatenate(y_parts, axis=1)
        y_inter = jnp.concatenate(
            [_dot(cm[:, g * SSD_STATE:(g + 1) * SSD_STATE].astype(BF16), ht_s[g].astype(BF16)) for g in range(SSD_GROUPS)],
            axis=1) * jnp.exp(cum_e)
        y = y_intra + y_inter + d_e * xs
        y = y * _silu(z_ref[0, pl.ds(r0, chunk), :])
        normed = []
        for g in range(SSD_GROUPS):
            yg = y[:, g * SSD_GROUP_W:(g + 1) * SSD_GROUP_W]
            normed.append(yg * lax.rsqrt(jnp.mean(yg * yg, axis=-1, keepdims=True) + EPS))
        y_ref[0, pl.ds(r0, chunk), :] = (jnp.concatenate(normed, axis=1) * nw).astype(BF16)

        xw = (xs * (jnp.exp(last_e - cum_e) * dt_e)).astype(BF16)
        for g in range(SSD_GROUPS):
            bm_t = bm[:, g * SSD_STATE:(g + 1) * SSD_STATE].T.astype(BF16)
            upd = _dot(bm_t, xw[:, g * SSD_GROUP_W:(g + 1) * SSD_GROUP_W])
            ht_s[g] = jnp.exp(last_e[:, g * SSD_GROUP_W:(g + 1) * SSD_GROUP_W]) * ht_s[g] + upd
        return carry

    lax.fori_loop(0, tl // chunk, chunk_body, 0)

    @pl.when(lt == n_lt - 1)
    def _():
        for g in range(SSD_GROUPS):
            h_out_ref[0, g * SSD_GROUP_W:(g + 1) * SSD_GROUP_W, :] = ht_s[g].T


def _ssd(xbc, z, dt, conv_w8, conv_b, dt_bias_pad, a_log_pad, a_log_e, d_e, norm_w, e_mat, conv0, h0, tl, chunk):
    b, l, _ = xbc.shape
    has_state = conv0 is not None
    per = tl // SUBLANES
    cur = lambda i, j: (i, j, 0)
    prev = lambda i, j: (i, jnp.maximum(j * per - 1, 0), 0)
    batch = lambda i, j: (i, 0, 0)
    consts = [conv_w8, conv_b, dt_bias_pad, a_log_pad, a_log_e, d_e, norm_w, e_mat]
    in_specs = [
        pl.BlockSpec((1, tl, SSD_CONV_DIM), cur),
        pl.BlockSpec((1, SUBLANES, SSD_CONV_DIM), prev),
        pl.BlockSpec((1, tl, SSD_INNER), cur),
        pl.BlockSpec((1, tl, LANES), cur),
    ] + [_const_spec(c.shape) for c in consts]
    args = [xbc, xbc, z, dt] + consts
    if has_state:
        in_specs += [pl.BlockSpec((1, SUBLANES, SSD_CONV_DIM), batch), pl.BlockSpec((1, SSD_INNER, SSD_STATE), batch)]
        args += [conv0, h0]
    return pl.pallas_call(
        functools.partial(_ssd_kernel, chunk=chunk, has_state=has_state),
        grid=(b, l // tl),
        in_specs=in_specs,
        out_specs=[
            pl.BlockSpec((1, tl, SSD_INNER), cur),
            pl.BlockSpec((1, SUBLANES, SSD_CONV_DIM), batch),
            pl.BlockSpec((1, SSD_INNER, SSD_STATE), batch),
        ],
        out_shape=(
            jax.ShapeDtypeStruct((b, l, SSD_INNER), BF16),
            jax.ShapeDtypeStruct((b, SUBLANES, SSD_CONV_DIM), F32),
            jax.ShapeDtypeStruct((b, SSD_INNER, SSD_STATE), F32),
        ),
        scratch_shapes=[
            pltpu.VMEM((tl + SUBLANES, SSD_CONV_DIM), F32),
            pltpu.VMEM((tl, SSD_CONV_DIM), F32),
            pltpu.VMEM((SSD_GROUPS, SSD_STATE, SSD_GROUP_W), F32),
        ],
        compiler_params=_params(("parallel", "arbitrary")),
        name="l0_ssd",
    )(*args)


def _ffn_ple(x, p, nf, wg, wu, wd, npl, wpg, wpp):
    xn = _rms(x, nf).astype(BF16)
    hid = (_silu(_dot(xn, wg)) * _dot(xn, wu)).astype(BF16)
    x = x + _dot(hid, wd)
    gate = _sigmoid(_dot(_rms(x, npl).astype(BF16), wpg))
    return x + gate * _dot(p.astype(BF16), wpp)


def _post0_kernel(x_ref, attn_ref, ssd_ref, p_ref, wo_a_ref, wo_s_ref, nf_ref, wg_ref, wu_ref, wd_ref, npl_ref,
                  wpg_ref, wpp_ref, o_ref):
    x = x_ref[...] + _dot(attn_ref[...], wo_a_ref[...]) + _dot(ssd_ref[...], wo_s_ref[...])
    o_ref[...] = _ffn_ple(x, p_ref[...], nf_ref[...], wg_ref[...], wu_ref[...], wd_ref[...], npl_ref[...],
                          wpg_ref[...], wpp_ref[...])


def _post0(x, attn, ssd, p, wo_a, wo_s, nf, wg, wu, wd, npl, wpg, wpp, tm):
    n = x.shape[0]
    row = lambda i: (i, 0)
    consts = [wo_a, wo_s, nf, wg, wu, wd, npl, wpg, wpp]
    return pl.pallas_call(
        _post0_kernel,
        grid=(n // tm,),
        in_specs=[
            pl.BlockSpec((tm, D_MODEL), row),
            pl.BlockSpec((tm, Q_DIM), row),
            pl.BlockSpec((tm, SSD_INNER), row),
            pl.BlockSpec((tm, PLE_DIM), row),
        ] + [_const_spec(c.shape) for c in consts],
        out_specs=pl.BlockSpec((tm, D_MODEL), row),
        out_shape=jax.ShapeDtypeStruct((n, D_MODEL), F32),
        compiler_params=_params(("parallel",)),
        name="l0_post",
    )(x, attn, ssd, p, *consts)


def _layer1_kernel(*refs, has_state):
    if has_state:
        (x_ref, p_ref, nm_ref, wi_ref, cw_ref, wo_ref, nf_ref, wg_ref, wu_ref, wd_ref, npl_ref, wpg_ref, wpp_ref, fn_ref,
         conv0_ref, o_ref, conv_out_ref, full_s) = refs
    else:
        (x_ref, p_ref, nm_ref, wi_ref, cw_ref, wo_ref, nf_ref, wg_ref, wu_ref, wd_ref, npl_ref, wpg_ref, wpp_ref, fn_ref,
         o_ref, conv_out_ref, full_s) = refs
    lt = pl.program_id(1)
    tl = x_ref.shape[1]
    pad = SUBLANES
    x = x_ref[0]
    y3 = _dot(_rms(x, nm_ref[...]).astype(BF16), wi_ref[...])
    bg = y3[:, :D_MODEL]
    gated = y3[:, D_MODEL:2 * D_MODEL] * y3[:, 2 * D_MODEL:]

    @pl.when(lt == 0)
    def _():
        if has_state:
            full_s[0:pad, :] = conv0_ref[0]
        else:
            full_s[0:pad, :] = jnp.zeros((pad, D_MODEL), F32)

    @pl.when(lt > 0)
    def _():
        full_s[0:pad, :] = full_s[tl:tl + pad, :]

    full_s[pad:pad + tl, :] = gated
    u = gated * cw_ref[SC_CONV - 1:SC_CONV, :]
    for j in range(SC_CONV - 1):
        off = pad - (SC_CONV - 1) + j
        u = u + full_s[off:off + tl, :] * cw_ref[j:j + 1, :]
    conv_out_ref[0] = full_s[tl:tl + pad, :]
    x = x + _dot((bg * u).astype(BF16), wo_ref[...])
    x = _ffn_ple(x, p_ref[0], nf_ref[...], wg_ref[...], wu_ref[...], wd_ref[...], npl_ref[...], wpg_ref[...], wpp_ref[...])
    o_ref[0] = _rms(x, fn_ref[...])


def _layer1(x, p, consts, conv0, tl):
    b, l, _ = x.shape
    has_state = conv0 is not None
    cur = lambda i, j: (i, j, 0)
    batch = lambda i, j: (i, 0, 0)
    in_specs = [pl.BlockSpec((1, tl, D_MODEL), cur), pl.BlockSpec((1, tl, PLE_DIM), cur)]
    in_specs += [_const_spec(c.shape) for c in consts]
    args = [x, p] + list(consts)
    if has_state:
        in_specs.append(pl.BlockSpec((1, SUBLANES, D_MODEL), batch))
        args.append(conv0)
    return pl.pallas_call(
        functools.partial(_layer1_kernel, has_state=has_state),
        grid=(b, l // tl),
        in_specs=in_specs,
        out_specs=[pl.BlockSpec((1, tl, D_MODEL), cur), pl.BlockSpec((1, SUBLANES, D_MODEL), batch)],
        out_shape=(jax.ShapeDtypeStruct((b, l, D_MODEL), F32), jax.ShapeDtypeStruct((b, SUBLANES, D_MODEL), F32)),
        scratch_shapes=[pltpu.VMEM((tl + SUBLANES, D_MODEL), F32)],
        compiler_params=_params(("parallel", "arbitrary")),
        name="l1_layer",
    )(*args)


def _pad_rows(a, rows):
    return jnp.pad(a, ((0, 0), (rows - a.shape[1], 0), (0, 0)))


def _trunk(x, p, pos0, cache_k, cache_v, ssm0, ssd_conv0, sc_conv0, w):
    b, l, _ = x.shape
    n = b * l
    tm = min(TOKEN_TILE, n)
    tl = min(TOKEN_TILE, l)
    chunk = min(CHUNK, l)
    decode = cache_k is not None

    cos_t, sin_t = _rope_tables(max(tm, l) if not decode else tm, pos0, l)
    q, k, v, z, xbc, dt = _inproj(x.reshape(n, D_MODEL), w["norm_mix0"], w["hyb_w_in"], cos_t, sin_t, tm)
    q = q.reshape(b, l, Q_DIM)
    k = k.reshape(b, l, KV_DIM)
    v = v.reshape(b, l, KV_DIM)
    if decode:
        attn = _attention(q, cache_k.reshape(b, WINDOW, KV_DIM), k, cache_v.reshape(b, WINDOW, KV_DIM), v,
                          w["sink_col_decode"], tl, chunk, mask_history=False)
        k_state, v_state = k, v
        conv0 = _pad_rows(ssd_conv0, SUBLANES)
        h0 = ssm0.reshape(b, SSD_INNER, SSD_STATE)
        sc0 = _pad_rows(sc_conv0, SUBLANES)
    else:
        attn = _attention(q, k, k, v, v, w["sink_col_prompt"], tl, chunk, mask_history=True)
        k_state, v_state = k[:, l - WINDOW:], v[:, l - WINDOW:]
        conv0 = h0 = sc0 = None
    ssd, conv_tail, h_new = _ssd(
        xbc.reshape(b, l, SSD_CONV_DIM), z.reshape(b, l, SSD_INNER), dt.reshape(b, l, LANES),
        w["ssd_conv_w"], w["ssd_conv_b"], w["ssd_dt_bias"], w["ssd_a_log"], w["ssd_a_log_e"], w["ssd_d_e"],
        w["ssd_norm_w"], w["expand"], conv0, h0, tl, chunk)
    x1 = _post0(x.reshape(n, D_MODEL), attn.reshape(n, Q_DIM), ssd.reshape(n, SSD_INNER), p[0].reshape(n, PLE_DIM),
                w["wo_attn"], w["wo_ssd"], w["norm_ffn0"], w["wg0"], w["wu0"], w["wd0"], w["norm_ple0"], w["wpg0"],
                w["wpp0"], tm)
    y, sc_tail = _layer1(x1.reshape(b, l, D_MODEL), p[1], w["layer1"], sc0, tl)
    heads = (b, -1, ATTN_KV_HEADS, HEAD_DIM)
    return (y, k_state.reshape(heads), v_state.reshape(heads),
            h_new.reshape(b, SSD_HEADS, SSD_HEAD_DIM, SSD_STATE),
            conv_tail[:, SUBLANES - (SSD_CONV - 1):], sc_tail[:, SUBLANES - (SC_CONV - 1):])


def _prepare_weights(norm_mix, norm_ffn, norm_ple, final_norm, hyb_w_in, hyb_w_out, attn_sinks, ssd_conv_w, ssd_conv_b,
                     ssd_dt_bias, ssd_a_log, ssd_d, ssd_norm_w, sc_w_in, sc_conv_w, sc_w_out, ffn_w_gate, ffn_w_up,
                     ffn_w_down, ple_w_proj, ple_w_gate, dec_seq):
    row = lambda a: a.reshape(1, -1).astype(F32)
    lane_pad = lambda a: jnp.pad(a.astype(F32), (0, LANES - a.shape[0])).reshape(1, LANES)
    per_head = lambda a: jnp.repeat(a.astype(F32), SSD_HEAD_DIM).reshape(1, SSD_INNER)
    sinks = attn_sinks.astype(F32).reshape(ATTN_KV_HEADS, ATTN_GROUP)
    sink_col = lambda rows: jnp.repeat(sinks, rows, axis=1).reshape(ATTN_KV_HEADS, ATTN_GROUP * rows, 1)
    expand = (jnp.arange(LANES)[:, None] == jnp.arange(SSD_INNER)[None, :] // SSD_HEAD_DIM).astype(BF16)
    pad8 = lambda a: jnp.pad(a.astype(F32), ((0, SUBLANES - a.shape[0]), (0, 0)))
    w = {
        "norm_mix0": row(norm_mix[0]),
        "hyb_w_in": jnp.pad(hyb_w_in, ((0, 0), (0, HYB_IN_PAD - hyb_w_in.shape[1]))).astype(BF16),
        "sink_col_prompt": sink_col(CHUNK),
        "sink_col_decode": sink_col(dec_seq),
        "ssd_conv_w": pad8(ssd_conv_w),
        "ssd_conv_b": row(ssd_conv_b),
        "ssd_dt_bias": lane_pad(ssd_dt_bias),
        "ssd_a_log": lane_pad(ssd_a_log),
        "ssd_a_log_e": per_head(ssd_a_log),
        "ssd_d_e": per_head(ssd_d),
        "ssd_norm_w": row(ssd_norm_w),
        "expand": expand,
        "wo_attn": hyb_w_out[:Q_DIM].astype(BF16),
        "wo_ssd": hyb_w_out[Q_DIM:].astype(BF16),
        "norm_ffn0": row(norm_ffn[0]),
        "wg0": ffn_w_gate[0].astype(BF16),
        "wu0": ffn_w_up[0].astype(BF16),
        "wd0": ffn_w_down[0].astype(BF16),
        "norm_ple0": row(norm_ple[0]),
        "wpg0": ple_w_gate[0].astype(BF16),
        "wpp0": ple_w_proj[0].astype(BF16),
        "layer1": [
            row(norm_mix[1]), sc_w_in.astype(BF16), pad8(sc_conv_w), sc_w_out.astype(BF16), row(norm_ffn[1]),
            ffn_w_gate[1].astype(BF16), ffn_w_up[1].astype(BF16), ffn_w_down[1].astype(BF16), row(norm_ple[1]),
            ple_w_gate[1].astype(BF16), ple_w_proj[1].astype(BF16), row(final_norm),
        ],
    }
    return w


def kernel(x_prompt, x_sample, p_prompt, p_sample, cache_win_k, cache_win_v, state_ssm, state_ssd_conv, state_short_conv, norm_mix, norm_ffn, norm_ple, final_norm, hyb_w_in, hyb_w_out, attn_sinks, ssd_conv_w, ssd_conv_b, ssd_dt_bias, ssd_a_log, ssd_d, ssd_norm_w, sc_w_in, sc_conv_w, sc_w_out, ffn_w_gate, ffn_w_up, ffn_w_down, ple_w_proj, ple_w_gate):
    w = _prepare_weights(norm_mix, norm_ffn, norm_ple, final_norm, hyb_w_in, hyb_w_out, attn_sinks, ssd_conv_w,
                         ssd_conv_b, ssd_dt_bias, ssd_a_log, ssd_d, ssd_norm_w, sc_w_in, sc_conv_w, sc_w_out,
                         ffn_w_gate, ffn_w_up, ffn_w_down, ple_w_proj, ple_w_gate, x_sample.shape[1])
    y_p, pk, pv, pssm, pconv, psc = _trunk(x_prompt, p_prompt, 0, None, None, None, None, None, w)
    y_s, sk, sv, sssm, sconv, ssc = _trunk(x_sample, p_sample, PAST_LEN, cache_win_k, cache_win_v, state_ssm,
                                           state_ssd_conv, state_short_conv, w)
    return (y_p, y_s, pk, pv, pssm, pconv, psc, sk, sv, sssm, sconv, ssc)
```
